```python
import math
import jax, jax.numpy as jnp
from jax import lax
import numpy as np

D_MODEL = 1024
BATCH = 2
SEQ = 8192
DEPTH = 2

GRID_W = 64
HEAD_DIM = 64
ATTN_HEADS = 8
ATTN_KV_HEADS = 2
ATTN_WIDTH = ATTN_HEADS * HEAD_DIM
KV_WIDTH = ATTN_KV_HEADS * HEAD_DIM
HY_HEADS = 4
HY_WIDTH = HY_HEADS * HEAD_DIM
HY_ORDER = 2
FN_GROUPS = 4
FN_WIDTH = FN_GROUPS * HEAD_DIM
D_MIX = ATTN_WIDTH + HY_WIDTH + FN_WIDTH
HY_PROJ = (HY_ORDER + 1) * HY_WIDTH
IN_COLS = ATTN_WIDTH + 2 * KV_WIDTH + HY_PROJ + FN_WIDTH
Q_BLOCK = 128
ROPE_THETA = 10000.0
SHORT_CONV = 3
HY_BANDS = 8
HY_POS_FEATS = 1 + 2 * HY_BANDS
HY_FILTER_WIDTH = 64
HY_FILTERS = HY_ORDER * 2 * HY_WIDTH
HY_FAST_DECAY_PCT = 0.3
HY_SLOW_DECAY_PCT = 1.5
HY_DECAY_TARGET = 1e-2
PEER_KEYS = 128
PEER_EXPERTS = PEER_KEYS * PEER_KEYS
PEER_HEADS = 8
PEER_DK = 256
PEER_TOPK = 16
PEER_TOKEN_BLOCK = 128
EPS = 1e-6

kernel_name = 'hybrid_attn_hyena_fnet_peer_encoder'


def rmsnorm(x, g):
    xf = x.astype(jnp.float32)
    y = xf * lax.rsqrt(jnp.mean(xf * xf, axis=-1, keepdims=True) + EPS)
    return (y * g.astype(jnp.float32)).astype(x.dtype)


def axial_rope_tables(L):
    rows_n = L // GRID_W
    row = jnp.repeat(jnp.arange(rows_n), GRID_W).astype(jnp.float32)
    col = jnp.tile(jnp.arange(GRID_W), rows_n).astype(jnp.float32)
    n_freq = HEAD_DIM // 4
    inv = ROPE_THETA ** (-jnp.arange(n_freq, dtype=jnp.float32) / n_freq)
    ang = jnp.stack([row[:, None] * inv, col[:, None] * inv], axis=1)
    return jnp.cos(ang), jnp.sin(ang)


def apply_axial_rope(x, cos, sin):
    B, L, H, _ = x.shape
    xr = x.astype(jnp.float32).reshape(B, L, H, 2, 2, HEAD_DIM // 4)
    c = cos[None, :, None]
    s = sin[None, :, None]
    x1, x2 = xr[..., 0, :], xr[..., 1, :]
    out = jnp.stack([x1 * c - x2 * s, x2 * c + x1 * s], axis=-2)
    return out.reshape(B, L, H, HEAD_DIM).astype(x.dtype)


def attention_group(q, k, v, q_g, k_g, cos, sin):
    B, L, _ = q.shape
    rep = ATTN_HEADS // ATTN_KV_HEADS
    q = apply_axial_rope(rmsnorm(q.reshape(B, L, ATTN_HEADS, HEAD_DIM), q_g), cos, sin)
    k = apply_axial_rope(rmsnorm(k.reshape(B, L, ATTN_KV_HEADS, HEAD_DIM), k_g), cos, sin)
    v = v.reshape(B, L, ATTN_KV_HEADS, HEAD_DIM)
    n_blk = L // Q_BLOCK
    qb = q.reshape(B, n_blk, Q_BLOCK, ATTN_KV_HEADS, rep, HEAD_DIM).transpose(1, 0, 2, 3, 4, 5)
    scale = HEAD_DIM ** -0.5

    def block(qi):
        s = jnp.einsum('bqgrd,bkgd->bgrqk', qi, k).astype(jnp.float32) * scale
        p = jax.nn.softmax(s, axis=-1).astype(v.dtype)
        return jnp.einsum('bgrqk,bkgd->bqgrd', p, v)

    out = lax.map(block, qb)
    return out.transpose(1, 0, 2, 3, 4, 5).reshape(B, L, ATTN_WIDTH)


def short_conv(u, w, b):
    L = u.shape[1]
    pad = SHORT_CONV // 2
    up = jnp.pad(u, ((0, 0), (pad, pad), (0, 0)))
    y = b
    for j in range(SHORT_CONV):
        y = y + up[:, j:j + L] * w[j]
    return y


def hyena_filter_spectra(L, w1, b1, f1, w2, b2, f2, w3):
    t = jnp.linspace(0.0, 1.0, L, dtype=jnp.float32)[:, None]
    bands = jnp.arange(1, HY_BANDS + 1, dtype=jnp.float32)
    ang = 2.0 * math.pi * t * bands
    z = jnp.concatenate([t, jnp.sin(ang), jnp.cos(ang)], axis=-1)
    h = jnp.sin(f1 * (z @ w1.astype(jnp.float32) + b1))
    h = jnp.sin(f2 * (h @ w2.astype(jnp.float32) + b2))
    h = (h @ w3.astype(jnp.float32)).reshape(L, HY_ORDER, 2, HY_WIDTH)
    rates = jnp.linspace(-math.log(HY_DECAY_TARGET) / HY_SLOW_DECAY_PCT,
                         -math.log(HY_DECAY_TARGET) / HY_FAST_DECAY_PCT, HY_WIDTH, dtype=jnp.float32)
    h = h * jnp.exp(-t[:, :, None, None] * rates)
    h = h * lax.rsqrt(jnp.sum(h * h, axis=(0, 2), keepdims=True) + EPS)
    fwd, bwd = h[:, :, 0], h[:, :, 1]
    k_full = jnp.concatenate([fwd, jnp.zeros_like(fwd[:1]), bwd[1:][::-1]], axis=0)
    return jnp.fft.rfft(k_full, n=2 * L, axis=0)


def long_conv(u, kf, bias):
    L = u.shape[1]
    uf = jnp.fft.rfft(u, n=2 * L, axis=1)
    y = jnp.fft.irfft(uf * kf[None], n=2 * L, axis=1)[:, :L]
    return y + u * bias


def hyena_group(proj, conv_w, conv_b, kf, bias):
    u = short_conv(proj, conv_w, conv_b).astype(jnp.float32)
    v, x1, x2 = jnp.split(u, HY_ORDER + 1, axis=-1)
    z = v
    for n, gate in enumerate((x1, x2)):
        z = gate * long_conv(z, kf[:, n], bias[n].astype(jnp.float32))
    return z


def fourier_group(u, w, b):
    B, L, _ = u.shape
    ug = u.astype(jnp.float32).reshape(B, L, FN_GROUPS, HEAD_DIM)
    f = jnp.fft.fft2(ug, axes=(1, 3), norm='ortho').real
    y = jnp.einsum('blgd,gde->blge', f, w.astype(jnp.float32)).reshape(B, L, FN_WIDTH)
    return y + b.astype(jnp.float32)


def peer(x, w_query, sub_keys, u_tab, v_tab):
    B, L, D = x.shape
    T = B * L
    xt = x.reshape(T, D)
    q = (xt @ w_query).reshape(T, PEER_HEADS, 2, PEER_DK // 2)
    s = jnp.einsum('thpd,phkd->thpk', q, sub_keys).astype(jnp.float32)
    s_top, i_top = lax.top_k(s, PEER_TOPK)
    cand = (s_top[:, :, 0, :, None] + s_top[:, :, 1, None, :]).reshape(T, PEER_HEADS, PEER_TOPK * PEER_TOPK)
    cand_idx = (i_top[:, :, 0, :, None] * PEER_KEYS + i_top[:, :, 1, None, :]).reshape(T, PEER_HEADS, PEER_TOPK * PEER_TOPK)
    best, pos = lax.top_k(cand, PEER_TOPK)
    idx = jnp.take_along_axis(cand_idx, pos, axis=-1)
    g = jax.nn.softmax(best, axis=-1)
    n_blk = T // PEER_TOKEN_BLOCK

    def block(args):
        xb, ib, gb = args
        ue = jnp.take(u_tab, ib, axis=0)
        a = jax.nn.gelu(jnp.einsum('thkd,td->thk', ue, xb).astype(jnp.float32), approximate=False)
        ve = jnp.take(v_tab, ib, axis=0)
        return jnp.einsum('thk,thkd->td', (gb * a).astype(ve.dtype), ve)

    out = lax.map(block, (xt.reshape(n_blk, PEER_TOKEN_BLOCK, D),
                          idx.reshape(n_blk, PEER_TOKEN_BLOCK, PEER_HEADS, PEER_TOPK),
                          g.reshape(n_blk, PEER_TOKEN_BLOCK, PEER_HEADS, PEER_TOPK)))
    return out.reshape(B, L, D).astype(x.dtype)


def setup_inputs(seed: int = 0) -> dict:
    key = jax.random.key(seed)
    ks = jax.random.split(key, 32)

    def nrm(k, shape, scale):
        return jax.random.normal(k, shape, jnp.float32) * scale

    def gain(k, shape):
        return 1.0 + 0.02 * jax.random.normal(k, shape, jnp.float32)

    return {
        'x': nrm(ks[0], (BATCH, SEQ, D_MODEL), 1.0),
        'norm1_g': gain(ks[1], (DEPTH, D_MODEL)),
        'w_in': nrm(ks[2], (DEPTH, D_MODEL, IN_COLS), D_MODEL ** -0.5),
        'q_norm_g': gain(ks[3], (DEPTH, HEAD_DIM)),
        'k_norm_g': gain(ks[4], (DEPTH, HEAD_DIM)),
        'hy_conv_w': nrm(ks[5], (DEPTH, SHORT_CONV, HY_PROJ), SHORT_CONV ** -0.5),
        'hy_conv_b': nrm(ks[6], (DEPTH, HY_PROJ), 0.02),
        'hy_w1': nrm(ks[7], (DEPTH, HY_POS_FEATS, HY_FILTER_WIDTH), HY_POS_FEATS ** -0.5),
        'hy_b1': nrm(ks[8], (DEPTH, HY_FILTER_WIDTH), 0.02),
        'hy_freq1': gain(ks[9], (DEPTH, HY_FILTER_WIDTH)),
        'hy_w2': nrm(ks[10], (DEPTH, HY_FILTER_WIDTH, HY_FILTER_WIDTH), HY_FILTER_WIDTH ** -0.5),
        'hy_b2': nrm(ks[11], (DEPTH, HY_FILTER_WIDTH), 0.02),
        'hy_freq2': gain(ks[12], (DEPTH, HY_FILTER_WIDTH)),
        'hy_w3': nrm(ks[13], (DEPTH, HY_FILTER_WIDTH, HY_FILTERS), HY_FILTER_WIDTH ** -0.5),
        'hy_bias': nrm(ks[14], (DEPTH, HY_ORDER, HY_WIDTH), 0.1),
        'fn_w': nrm(ks[15], (DEPTH, FN_GROUPS, HEAD_DIM, HEAD_DIM), HEAD_DIM ** -0.5),
        'fn_b': nrm(ks[16], (DEPTH, FN_WIDTH), 0.02),
        'mix_norm_g': gain(ks[17], (DEPTH, D_MIX)),
        'w_out': nrm(ks[18], (DEPTH, D_MIX, D_MODEL), D_MIX ** -0.5),
        'norm2_g': gain(ks[19], (DEPTH, D_MODEL)),
        'peer_w_query': nrm(ks[20], (DEPTH, D_MODEL, PEER_HEADS * PEER_DK), D_MODEL ** -0.5),
        'peer_sub_keys': nrm(ks[21], (DEPTH, 2, PEER_HEADS, PEER_KEYS, PEER_DK // 2), (PEER_DK // 2) ** -0.5),
        'peer_u': nrm(ks[22], (DEPTH, PEER_EXPERTS, D_MODEL), D_MODEL ** -0.5),
        'peer_v': nrm(ks[23], (DEPTH, PEER_EXPERTS, D_MODEL), PEER_HEADS ** -0.5),
        'final_g': gain(ks[24], (D_MODEL,)),
    }


def reference(x, norm1_g, w_in, q_norm_g, k_norm_g, hy_conv_w, hy_conv_b, hy_w1, hy_b1, hy_freq1,
              hy_w2, hy_b2, hy_freq2, hy_w3, hy_bias, fn_w, fn_b, mix_norm_g, w_out, norm2_g,
              peer_w_query, peer_sub_keys, peer_u, peer_v, final_g):
    L = x.shape[1]
    cos, sin = axial_rope_tables(L)
    o_k = ATTN_WIDTH
    o_v = o_k + KV_WIDTH
    o_h = o_v + KV_WIDTH
    o_f = o_h + HY_PROJ
    for l in range(DEPTH):
        h = rmsnorm(x, norm1_g[l])
        p = h @ w_in[l]
        a = attention_group(p[..., :o_k], p[..., o_k:o_v], p[..., o_v:o_h],
                            q_norm_g[l], k_norm_g[l], cos, sin)
        kf = hyena_filter_spectra(L, hy_w1[l], hy_b1[l], hy_freq1[l], hy_w2[l], hy_b2[l],
                                  hy_freq2[l], hy_w3[l])
        y = hyena_group(p[..., o_h:o_f], hy_conv_w[l], hy_conv_b[l], kf, hy_bias[l])
        f = fourier_group(p[..., o_f:], fn_w[l], fn_b[l])
        g = mix_norm_g[l]
        mixed = jnp.concatenate([
            rmsnorm(a, g[:ATTN_WIDTH]),
            rmsnorm(y.astype(x.dtype), g[ATTN_WIDTH:ATTN_WIDTH + HY_WIDTH]),
            rmsnorm(f.astype(x.dtype), g[ATTN_WIDTH + HY_WIDTH:]),
        ], axis=-1)
        x = x + mixed @ w_out[l]
        x = x + peer(rmsnorm(x, norm2_g[l]), peer_w_query[l], peer_sub_keys[l], peer_u[l], peer_v[l])
    return rmsnorm(x, final_g)
```

```python
import functools
import math

import jax
import jax.numpy as jnp
from jax import lax
from jax.experimental import pallas as pl
from jax.experimental.pallas import tpu as pltpu

F32 = jnp.float32
BF16 = jnp.bfloat16

D_MODEL = 1024
GRID_W = 64
HEAD_DIM = 64
ATTN_HEADS = 8
ATTN_KV_HEADS = 2
ATTN_REP = ATTN_HEADS // ATTN_KV_HEADS
ATTN_WIDTH = ATTN_HEADS * HEAD_DIM
KV_WIDTH = ATTN_KV_HEADS * HEAD_DIM
HY_WIDTH = 256
HY_ORDER = 2
FN_GROUPS = 4
FN_WIDTH = 256
HY_PROJ = (HY_ORDER + 1) * HY_WIDTH
ROPE_THETA = 10000.0
SHORT_CONV = 3
HY_BANDS = 8
HY_FAST_DECAY_PCT = 0.3
HY_SLOW_DECAY_PCT = 1.5
HY_DECAY_TARGET = 1e-2
PEER_KEYS = 128
PEER_HEADS = 8
PEER_DK = 256
PEER_TOPK = 16
EPS = 1e-6

V7X_VMEM_BYTES = 64 * 1024 * 1024
VMEM_LIMIT_BYTES = V7X_VMEM_BYTES - 8 * 1024 * 1024
LANES = 128

ATTN_TQ = 512
ATTN_TK = 512
V_ROWS = HEAD_DIM + 8
PEER_TQ = 512
PEER_EBLK = 2 * PEER_KEYS


def _params(*sem):
    return pltpu.CompilerParams(dimension_semantics=sem, vmem_limit_bytes=VMEM_LIMIT_BYTES)


def _rms_scale(x):
    return lax.rsqrt(jnp.mean(x * x, axis=-1, keepdims=True) + EPS)


def _rms_matmul_body(x_ref, g_ref, w_ref, o_ref):
    x = x_ref[...]
    xn = (x * _rms_scale(x) * g_ref[...]).astype(BF16)
    o_ref[...] = jnp.dot(xn, w_ref[...], preferred_element_type=F32)


def rms_matmul(x, g, w, tm=512):
    T, D = x.shape
    N = w.shape[1]
    return pl.pallas_call(
        _rms_matmul_body,
        grid=(T // tm,),
        in_specs=[
            pl.BlockSpec((tm, D), lambda i: (i, 0)),
            pl.BlockSpec((1, D), lambda i: (0, 0)),
            pl.BlockSpec((D, N), lambda i: (0, 0)),
        ],
        out_specs=pl.BlockSpec((tm, N), lambda i: (i, 0)),
        out_shape=jax.ShapeDtypeStruct((T, N), F32),
        compiler_params=_params("arbitrary"),
        name="rms_matmul",
    )(x, g.reshape(1, D), w)


def _attn_body(qT_ref, k_ref, vT_ref, o_ref, m_sc, acc_sc):
    kv = pl.program_id(3)

    @pl.when(kv == 0)
    def _():
        m_sc[...] = jnp.full(m_sc.shape, -jnp.inf, F32)
        acc_sc[...] = jnp.zeros(acc_sc.shape, F32)

    k = k_ref[0, 0]
    vT = vT_ref[0, 0]
    for r in range(ATTN_REP):
        s = jnp.dot(k, qT_ref[0, 0, r], preferred_element_type=F32)
        m_old = m_sc[r]
        m_new = jnp.maximum(m_old, jnp.max(s, axis=0, keepdims=True))
        alpha = jnp.exp2(m_old - m_new)
        p = jnp.exp2(s - m_new).astype(BF16)
        acc_sc[r] = alpha * acc_sc[r] + jnp.dot(vT, p, preferred_element_type=F32)
        m_sc[r] = m_new

    @pl.when(kv == pl.num_programs(3) - 1)
    def _():
        for r in range(ATTN_REP):
            acc = acc_sc[r]
            o_ref[0, 0, r] = acc[:HEAD_DIM] / acc[HEAD_DIM:HEAD_DIM + 1]


def attention(qT, k, vT):
    B, G, R, HD, L = qT.shape
    tq, tk = ATTN_TQ, ATTN_TK
    return pl.pallas_call(
        _attn_body,
        grid=(B, G, L // tq, L // tk),
        in_specs=[
            pl.BlockSpec((1, 1, R, HD, tq), lambda b, g, i, j: (b, g, 0, 0, i)),
            pl.BlockSpec((1, 1, tk, HD), lambda b, g, i, j: (b, g, j, 0)),
            pl.BlockSpec((1, 1, V_ROWS, tk), lambda b, g, i, j: (b, g, 0, j)),
        ],
        out_specs=pl.BlockSpec((1, 1, R, HD, tq), lambda b, g, i, j: (b, g, 0, 0, i)),
        out_shape=jax.ShapeDtypeStruct((B, G, R, HD, L), F32),
        scratch_shapes=[
            pltpu.VMEM((R, 1, tq), F32),
            pltpu.VMEM((R, V_ROWS, tq), F32),
        ],
        compiler_params=_params("arbitrary", "arbitrary", "arbitrary", "arbitrary"),
        name="attention",
    )(qT, k, vT)


def _mix_body(x_ref, a_ref, y_ref, f_ref, g_ref, w_ref, o_ref):
    g = g_ref[...]
    o1 = ATTN_WIDTH
    o2 = ATTN_WIDTH + HY_WIDTH

    def nrm(v, gg):
        return (v * _rms_scale(v) * gg).astype(BF16)

    acc = x_ref[...]
    acc = acc + jnp.dot(nrm(a_ref[...], g[:, :o1]), w_ref[:o1, :], preferred_element_type=F32)
    acc = acc + jnp.dot(nrm(y_ref[...], g[:, o1:o2]), w_ref[o1:o2, :], preferred_element_type=F32)
    acc = acc + jnp.dot(nrm(f_ref[...], g[:, o2:]), w_ref[o2:, :], preferred_element_type=F32)
    o_ref[...] = acc


def mix_out(x, a, y, f, g, w, tm=512):
    T, D = x.shape
    return pl.pallas_call(
        _mix_body,
        grid=(T // tm,),
        in_specs=[
            pl.BlockSpec((tm, D), lambda i: (i, 0)),
            pl.BlockSpec((tm, a.shape[1]), lambda i: (i, 0)),
            pl.BlockSpec((tm, y.shape[1]), lambda i: (i, 0)),
            pl.BlockSpec((tm, f.shape[1]), lambda i: (i, 0)),
            pl.BlockSpec((1, g.shape[0]), lambda i: (0, 0)),
            pl.BlockSpec(w.shape, lambda i: (0, 0)),
        ],
        out_specs=pl.BlockSpec((tm, D), lambda i: (i, 0)),
        out_shape=jax.ShapeDtypeStruct((T, D), F32),
        compiler_params=_params("arbitrary"),
        name="mix_out",
    )(x, a, y, f, g.reshape(1, -1), w)


def _peer_front_body(x_ref, g_ref, wq_ref, keys_ref, xn_ref, sT_ref):
    x = x_ref[...]
    xn = (x * _rms_scale(x) * g_ref[...]).astype(BF16)
    xn_ref[...] = xn
    q = jnp.dot(xn, wq_ref[...], preferred_element_type=F32)
    half = PEER_DK // 2
    for hp in range(2 * PEER_HEADS):
        qhp = q[:, hp * half:(hp + 1) * half].astype(BF16)
        sT_ref[hp] = lax.dot_general(keys_ref[hp], qhp, (((1,), (1,)), ((), ())), preferred_element_type=F32)


def peer_front(x, g, wq, keys, tm=512):
    T, D = x.shape
    NQ = wq.shape[1]
    HP, NK, DK = keys.shape
    return pl.pallas_call(
        _peer_front_body,
        grid=(T // tm,),
        in_specs=[
            pl.BlockSpec((tm, D), lambda i: (i, 0)),
            pl.BlockSpec((1, D), lambda i: (0, 0)),
            pl.BlockSpec((D, NQ), lambda i: (0, 0)),
            pl.BlockSpec((HP, NK, DK), lambda i: (0, 0, 0)),
        ],
        out_specs=[
            pl.BlockSpec((tm, D), lambda i: (i, 0)),
            pl.BlockSpec((HP, NK, tm), lambda i: (0, 0, i)),
        ],
        out_shape=[
            jax.ShapeDtypeStruct((T, D), BF16),
            jax.ShapeDtypeStruct((HP, NK, T), F32),
        ],
        compiler_params=_params("arbitrary"),
        name="peer_front",
    )(x, g.reshape(1, D), wq, keys)


def _gelu_exact(a):
    return 0.5 * a * (1.0 + lax.erf(a * (1.0 / math.sqrt(2.0))))


def _peer_main_body(x_ref, xn_ref, u_ref, v_ref, e1_ref, phi_ref, e2_ref, o_ref):
    ib = pl.program_id(1)

    @pl.when(ib == 0)
    def _():
        o_ref[...] = x_ref[...]

    tq = xn_ref.shape[0]
    aT = lax.dot_general(u_ref[...], xn_ref[...], (((1,), (1,)), ((), ())), preferred_element_type=F32)
    cols = []
    for c in range(tq // LANES):
        lane = pl.ds(c * LANES, LANES)
        halves = []
        for half in range(2):
            i = 2 * ib + half
            ph_i = phi_ref[i, :, lane]
            e1_i = e1_ref[i, :, lane]
            w = jnp.zeros((PEER_KEYS, LANES), F32)
            for h in range(PEER_HEADS):
                e2 = e2_ref[h, :, lane]
                w = w + jnp.where(e2 >= ph_i[h:h + 1], e2, 0.0) * e1_i[h:h + 1]
            halves.append(w)
        wt = jnp.concatenate(halves, axis=0)
        a = aT[:, c * LANES:(c + 1) * LANES]
        cols.append((_gelu_exact(a) * wt).astype(BF16))
    wa = jnp.concatenate(cols, axis=1)
    o_ref[...] += lax.dot_general(wa, v_ref[...], (((0,), (0,)), ((), ())), preferred_element_type=F32)


def peer_main(x, xn, u16, v16, e1T, phiT, e2T):
    T, D = x.shape
    E = u16.shape[0]
    tq = PEER_TQ
    H, NK, _ = e2T.shape
    aux1 = pl.BlockSpec((NK, H, tq), lambda t, e: (0, 0, t))
    aux2 = pl.BlockSpec((H, NK, tq), lambda t, e: (0, 0, t))
    return pl.pallas_call(
        _peer_main_body,
        grid=(T // tq, E // PEER_EBLK),
        in_specs=[
            pl.BlockSpec((tq, D), lambda t, e: (t, 0)),
            pl.BlockSpec((tq, D), lambda t, e: (t, 0)),
            pl.BlockSpec((PEER_EBLK, D), lambda t, e: (e, 0)),
            pl.BlockSpec((PEER_EBLK, D), lambda t, e: (e, 0)),
            aux1, aux1, aux2,
        ],
        out_specs=pl.BlockSpec((tq, D), lambda t, e: (t, 0)),
        out_shape=jax.ShapeDtypeStruct((T, D), F32),
        compiler_params=_params("arbitrary", "arbitrary"),
        name="peer_main",
    )(x, xn, u16, v16, e1T, phiT, e2T)


def _peer_select(sT):
    HP, NK, T = sT.shape
    s = sT.reshape(PEER_HEADS, 2, NK, T)
    s1, s2 = s[:, 0], s[:, 1]
    k1 = PEER_TOPK + 1
    t1 = lax.top_k(s1.transpose(0, 2, 1), k1)[0]
    t2 = lax.top_k(s2.transpose(0, 2, 1), k1)[0]
    cand = (t1[..., :, None] + t2[..., None, :]).reshape(PEER_HEADS, T, k1 * k1)
    best = lax.top_k(cand, k1)[0]
    thr = 0.5 * (best[..., PEER_TOPK - 1] + best[..., PEER_TOPK])
    m1, m2 = t1[..., 0], t2[..., 0]
    z = jnp.sum(jnp.exp(best[..., :PEER_TOPK] - (m1 + m2)[..., None]), axis=-1)
    e1T = jnp.exp(s1 - m1[:, None, :]) / z[:, None, :]
    e2T = jnp.exp(s2 - m2[:, None, :])
    phiT = jnp.exp(thr[:, None, :] - s1 - m2[:, None, :])
    return e1T.transpose(1, 0, 2), phiT.transpose(1, 0, 2), e2T


def _rmsnorm_x(x, g):
    return x * lax.rsqrt(jnp.mean(x * x, axis=-1, keepdims=True) + EPS) * g


def _rope_tables(L):
    rows_n = L // GRID_W
    row = jnp.repeat(jnp.arange(rows_n), GRID_W).astype(F32)
    col = jnp.tile(jnp.arange(GRID_W), rows_n).astype(F32)
    n_freq = HEAD_DIM // 4
    inv = ROPE_THETA ** (-jnp.arange(n_freq, dtype=F32) / n_freq)
    ang = jnp.stack([row[:, None] * inv, col[:, None] * inv], axis=1)
    return jnp.cos(ang), jnp.sin(ang)


def _rope(x, cos, sin):
    B, L, H, _ = x.shape
    xr = x.reshape(B, L, H, 2, 2, HEAD_DIM // 4)
    c = cos[None, :, None]
    s = sin[None, :, None]
    x1, x2 = xr[..., 0, :], xr[..., 1, :]
    out = jnp.stack([x1 * c - x2 * s, x2 * c + x1 * s], axis=-2)
    return out.reshape(B, L, H, HEAD_DIM)


def _short_conv(u, w, b):
    L = u.shape[1]
    pad = SHORT_CONV // 2
    up = jnp.pad(u, ((0, 0), (pad, pad), (0, 0)))
    y = b
    for j in range(SHORT_CONV):
        y = y + up[:, j:j + L] * w[j]
    return y


def _hyena_spectra(L, w1, b1, f1, w2, b2, f2, w3):
    t = jnp.linspace(0.0, 1.0, L, dtype=F32)[:, None]
    bands = jnp.arange(1, HY_BANDS + 1, dtype=F32)
    ang = 2.0 * math.pi * t * bands
    z = jnp.concatenate([t, jnp.sin(ang), jnp.cos(ang)], axis=-1)
    h = jnp.sin(f1 * (z @ w1 + b1))
    h = jnp.sin(f2 * (h @ w2 + b2))
    h = (h @ w3).reshape(L, HY_ORDER, 2, HY_WIDTH)
    rates = jnp.linspace(-math.log(HY_DECAY_TARGET) / HY_SLOW_DECAY_PCT,
                         -math.log(HY_DECAY_TARGET) / HY_FAST_DECAY_PCT, HY_WIDTH, dtype=F32)
    h = h * jnp.exp(-t[:, :, None, None] * rates)
    h = h * lax.rsqrt(jnp.sum(h * h, axis=(0, 2), keepdims=True) + EPS)
    fwd, bwd = h[:, :, 0], h[:, :, 1]
    k_full = jnp.concatenate([fwd, jnp.zeros_like(fwd[:1]), bwd[1:][::-1]], axis=0)
    return jnp.fft.rfft(k_full, n=2 * L, axis=0)


def _long_conv(u, kf, bias):
    L = u.shape[1]
    uf = jnp.fft.rfft(u, n=2 * L, axis=1)
    y = jnp.fft.irfft(uf * kf[None], n=2 * L, axis=1)[:, :L]
    return y + u * bias


def _hyena(proj, conv_w, conv_b, kf, bias):
    u = _short_conv(proj, conv_w, conv_b)
    v, x1, x2 = jnp.split(u, HY_ORDER + 1, axis=-1)
    z = v
    for n, gate in enumerate((x1, x2)):
        z = gate * _long_conv(z, kf[:, n], bias[n])
    return z


def _fourier(u, w, b):
    B, L, _ = u.shape
    ug = u.reshape(B, L, FN_GROUPS, HEAD_DIM)
    f = jnp.fft.fft2(ug, axes=(1, 3), norm='ortho').real
    y = jnp.einsum('blgd,gde->blge', f, w).reshape(B, L, FN_WIDTH)
    return y + b


def kernel(x, norm1_g, w_in, q_norm_g, k_norm_g, hy_conv_w, hy_conv_b, hy_w1, hy_b1, hy_freq1, hy_w2, hy_b2, hy_freq2, hy_w3, hy_bias, fn_w, fn_b, mix_norm_g, w_out, norm2_g, peer_w_query, peer_sub_keys, peer_u, peer_v, final_g):
    B, L, D = x.shape
    T = B * L
    depth = w_in.shape[0]
    cos, sin = _rope_tables(L)
    o_k = ATTN_WIDTH
    o_v = o_k + KV_WIDTH
    o_h = o_v + KV_WIDTH
    o_f = o_h + HY_PROJ
    q_scale = HEAD_DIM ** -0.5 * math.log2(math.e)

    xt = x.reshape(T, D)
    for l in range(depth):
        p = rms_matmul(xt, norm1_g[l], w_in[l].astype(BF16)).reshape(B, L, -1)

        q = _rmsnorm_x(p[..., :o_k].reshape(B, L, ATTN_HEADS, HEAD_DIM), q_norm_g[l])
        q = _rope(q, cos, sin) * q_scale
        k = _rmsnorm_x(p[..., o_k:o_v].reshape(B, L, ATTN_KV_HEADS, HEAD_DIM), k_norm_g[l])
        k = _rope(k, cos, sin)
        v = p[..., o_v:o_h].reshape(B, L, ATTN_KV_HEADS, HEAD_DIM)
        qT = q.reshape(B, L, ATTN_KV_HEADS, ATTN_REP, HEAD_DIM).transpose(0, 2, 3, 4, 1).astype(BF16)
        kk = k.transpose(0, 2, 1, 3).astype(BF16)
        vT = v.transpose(0, 2, 3, 1)
        vT = jnp.concatenate([vT, jnp.ones((B, ATTN_KV_HEADS, 1, L), F32),
                              jnp.zeros((B, ATTN_KV_HEADS, V_ROWS - HEAD_DIM - 1, L), F32)], axis=2).astype(BF16)
        oT = attention(qT, kk, vT)
        a = oT.transpose(0, 4, 1, 2, 3).reshape(T, ATTN_WIDTH)

        kf = _hyena_spectra(L, hy_w1[l], hy_b1[l], hy_freq1[l], hy_w2[l], hy_b2[l], hy_freq2[l], hy_w3[l])
        y = _hyena(p[..., o_h:o_f], hy_conv_w[l], hy_conv_b[l], kf, hy_bias[l]).reshape(T, HY_WIDTH)
        f = _fourier(p[..., o_f:], fn_w[l], fn_b[l]).reshape(T, FN_WIDTH)

        x1 = mix_out(xt, a, y, f, mix_norm_g[l], w_out[l].astype(BF16))

        keys = peer_sub_keys[l].transpose(1, 0, 2, 3).reshape(2 * PEER_HEADS, PEER_KEYS, PEER_DK // 2).astype(BF16)
        xn, sT = peer_front(x1, norm2_g[l], peer_w_query[l].astype(BF16), keys)
        e1T, phiT, e2T = _peer_select(sT)
        xt = peer_main(x1, xn, peer_u[l].astype(BF16), peer_v[l].astype(BF16), e1T, phiT, e2T)

    out = _rmsnorm_x(xt, final_g)
    return out.reshape(B, L, D)
```
